```python
import math
import jax, jax.numpy as jnp
from jax import lax
import numpy as np

D_MODEL = 1024
BATCH = 8
SEQ = 2048
DEPTH = 4

N_META = 16
N_A_LAYERS = DEPTH // 2
N_B_LAYERS = DEPTH - N_A_LAYERS
SSM_GROUP = 16
SSM_GROUPS = D_MODEL // SSM_GROUP
SSM_STATE = 64
DT_MIN = 1e-3
DT_MAX = 1e-1
HEAD_DIM = 64
N_Q_HEADS = D_MODEL // HEAD_DIM
N_KV_HEADS = 4
GQA_GROUP = N_Q_HEADS // N_KV_HEADS
KV_DIM = N_KV_HEADS * HEAD_DIM
WINDOW = 128
BLOCK = 128
PAD_FRONT = BLOCK - N_META
NEG = -1e30
N_EXPERT_GROUPS = 4
EXPERTS_PER_GROUP = 8
N_EXPERTS = N_EXPERT_GROUPS * EXPERTS_PER_GROUP
TOP_K = 2
D_EXPERT = D_MODEL // 2
MOE_BLOCK = 128
RMS_EPS = 1e-6

kernel_name = 'yoco_s5_swa_sink_hier_moe'

F32 = jnp.float32


def rmsnorm(x, g):
    xf = x.astype(F32)
    y = xf * lax.rsqrt(jnp.mean(xf * xf, axis=-1, keepdims=True) + RMS_EPS)
    return (y * g.astype(F32)).astype(x.dtype)


def s5_mixer(u, a_re, a_im, log_dt, b_re, b_im, c_re, c_im, d_skip, w_glu):
    bsz, L, _ = u.shape
    uf = u.astype(F32)
    ug = uf.reshape(bsz, L, SSM_GROUPS, SSM_GROUP).astype(jnp.complex64)
    lam = lax.complex(a_re.astype(F32), a_im.astype(F32))
    dt = jnp.exp(log_dt.astype(F32))[:, None]
    a_bar = jnp.exp(lam * dt)
    b = lax.complex(b_re.astype(F32), b_im.astype(F32))
    b_bar = ((a_bar - 1.0) / lam)[:, :, None] * b
    bu = jnp.einsum('blgc,gpc->blgp', ug, b_bar)
    a_seq = jnp.broadcast_to(a_bar, bu.shape)

    def combine(e1, e2):
        a1, s1 = e1
        a2, s2 = e2
        return a2 * a1, a2 * s1 + s2

    _, states = lax.associative_scan(combine, (a_seq, bu), axis=1)
    c = lax.complex(c_re.astype(F32), c_im.astype(F32))
    y = jnp.einsum('blgp,gcp->blgc', states, c).real.reshape(bsz, L, D_MODEL)
    z = jax.nn.gelu(y + d_skip.astype(F32) * uf).astype(u.dtype)
    val, gate = jnp.split(z @ w_glu, 2, axis=-1)
    return val * jax.nn.sigmoid(gate)


def to_blocks(t):
    t = jnp.pad(t, ((0, 0), (PAD_FRONT, 0)) + ((0, 0),) * (t.ndim - 2))
    return t.reshape((t.shape[0], t.shape[1] // BLOCK, BLOCK) + t.shape[2:])


def band_keys(tb):
    prev = jnp.pad(tb, ((0, 0), (1, 0), (0, 0), (0, 0), (0, 0)))[:, :-1]
    return jnp.concatenate([prev, tb], axis=2)


def shared_kv(h, w_kv, b_kv):
    bsz, L, _ = h.shape
    k, v = jnp.split(h @ w_kv + b_kv, 2, axis=-1)
    k = k.reshape(bsz, L, N_KV_HEADS, HEAD_DIM)
    v = v.reshape(bsz, L, N_KV_HEADS, HEAD_DIM)
    k_band = band_keys(to_blocks(k))
    v_band = band_keys(to_blocks(v))
    return k_band, v_band, k[:, :N_META], v[:, :N_META]


def swa_sink_attention(h, w_q, b_q, sinks, w_o, b_o, k_band, v_band, k_meta, v_meta):
    bsz, L, _ = h.shape
    q = (h @ w_q + b_q).reshape(bsz, L, N_KV_HEADS, GQA_GROUP, HEAD_DIM)
    qb = to_blocks(q)
    nb = qb.shape[1]
    scale = HEAD_DIM ** -0.5
    s_band = jnp.einsum('bnqkgd,bnskd->bnkgqs', qb, k_band).astype(F32) * scale
    s_meta = jnp.einsum('bnqkgd,bmkd->bnkgqm', qb, k_meta).astype(F32) * scale
    qpos = jnp.arange(nb)[:, None] * BLOCK + jnp.arange(BLOCK)[None, :]
    kpos = jnp.arange(nb)[:, None] * BLOCK - BLOCK + jnp.arange(2 * BLOCK)[None, :]
    diff = qpos[:, :, None] - kpos[:, None, :]
    band_ok = (kpos[:, None, :] >= BLOCK) & (diff >= 0) & (diff < WINDOW)
    mpos = PAD_FRONT + jnp.arange(N_META)
    meta_ok = mpos[None, None, :] <= qpos[:, :, None]
    s_band = jnp.where(band_ok[None, :, None, None], s_band, NEG)
    s_meta = jnp.where(meta_ok[None, :, None, None], s_meta, NEG)
    sink = sinks.astype(F32).reshape(1, 1, N_KV_HEADS, GQA_GROUP, 1, 1)
    sink = jnp.broadcast_to(sink, s_meta.shape[:-1] + (1,))
    probs = jax.nn.softmax(jnp.concatenate([sink, s_meta, s_band], axis=-1), axis=-1)
    p_meta = probs[..., 1:1 + N_META].astype(v_band.dtype)
    p_band = probs[..., 1 + N_META:].astype(v_band.dtype)
    o = (jnp.einsum('bnkgqm,bmkd->bnqkgd', p_meta, v_meta)
         + jnp.einsum('bnkgqs,bnskd->bnqkgd', p_band, v_band))
    o = o.reshape(bsz, nb * BLOCK, D_MODEL)[:, PAD_FRONT:]
    return o @ w_o + b_o


def hier_moe(h, w_grp, b_grp, w_exp, b_exp, w_gate, w_up, w_down):
    bsz, L, D = h.shape
    t = h.reshape(-1, D)
    T = t.shape[0]
    grp_prob = jax.nn.softmax((t @ w_grp + b_grp).astype(F32), axis=-1)
    p_g, g_idx = lax.top_k(grp_prob, 1)
    exp_logits = (t @ w_exp + b_exp).astype(F32).reshape(T, N_EXPERT_GROUPS, EXPERTS_PER_GROUP)
    sel_idx = jnp.broadcast_to(g_idx[:, :, None], (T, 1, EXPERTS_PER_GROUP))
    sel_logits = jnp.take_along_axis(exp_logits, sel_idx, axis=1)[:, 0]
    q_top, e_local = lax.top_k(jax.nn.softmax(sel_logits, axis=-1), TOP_K)
    q_top = q_top / jnp.sum(q_top, axis=-1, keepdims=True)
    gates = p_g * q_top
    experts = g_idx * EXPERTS_PER_GROUP + e_local
    n_assign = T * TOP_K
    flat_e = experts.reshape(-1)
    flat_tok = jnp.repeat(jnp.arange(T), TOP_K)
    flat_gate = gates.reshape(-1)
    order = jnp.argsort(flat_e)
    se = flat_e[order]
    stok = flat_tok[order]
    sgate = flat_gate[order]
    counts = jnp.bincount(flat_e, length=N_EXPERTS)
    padded = (counts + MOE_BLOCK - 1) // MOE_BLOCK * MOE_BLOCK
    pad_end = jnp.cumsum(padded)
    pad_start = pad_end - padded
    start = jnp.cumsum(counts) - counts
    dest = pad_start[se] + jnp.arange(n_assign) - start[se]
    n_rows = (n_assign // MOE_BLOCK + 1 + N_EXPERTS) * MOE_BLOCK
    n_blk = n_rows // MOE_BLOCK
    row_tok = jnp.full((n_rows,), T, dtype=jnp.int32).at[dest].set(stok)
    t_pad = jnp.concatenate([t, jnp.zeros((1, D), t.dtype)], axis=0)
    xb = t_pad[row_tok].reshape(n_blk, MOE_BLOCK, D)
    blk_e = jnp.minimum(jnp.searchsorted(pad_end, jnp.arange(n_blk) * MOE_BLOCK, side='right'),
                        N_EXPERTS - 1)

    def run_block(args):
        xblk, e = args
        hid = jax.nn.silu(xblk @ w_gate[e]) * (xblk @ w_up[e])
        return hid @ w_down[e]

    yb = lax.map(run_block, (xb, blk_e)).reshape(n_rows, D)
    contrib = yb[dest] * sgate[:, None].astype(yb.dtype)
    out = jnp.zeros_like(t).at[stok].add(contrib)
    return out.reshape(bsz, L, D)


def setup_inputs(seed: int = 0) -> dict:
    key = jax.random.key(seed)
    ks = iter(jax.random.split(key, 40))
    nrm = lambda shape, s: jax.random.normal(next(ks), shape, F32) * s
    G, P, C = SSM_GROUPS, SSM_STATE, SSM_GROUP
    a_im0 = jnp.pi * jnp.arange(P, dtype=F32)
    return {
        'x': nrm((BATCH, SEQ, D_MODEL), 1.0),
        'meta_tokens': nrm((N_META, D_MODEL), 1.0),
        'norm_mix': 1.0 + nrm((DEPTH, D_MODEL), 0.02),
        'norm_ffn': 1.0 + nrm((DEPTH, D_MODEL), 0.02),
        'norm_kv': 1.0 + nrm((D_MODEL,), 0.02),
        'norm_final': 1.0 + nrm((D_MODEL,), 0.02),
        'ssm_a_re': -0.5 * jnp.exp(nrm((N_A_LAYERS, G, P), 0.05)),
        'ssm_a_im': a_im0 + nrm((N_A_LAYERS, G, P), 0.01),
        'ssm_log_dt': jax.random.uniform(next(ks), (N_A_LAYERS, G), F32,
                                         minval=math.log(DT_MIN), maxval=math.log(DT_MAX)),
        'ssm_b_re': nrm((N_A_LAYERS, G, P, C), (2 * C) ** -0.5),
        'ssm_b_im': nrm((N_A_LAYERS, G, P, C), (2 * C) ** -0.5),
        'ssm_c_re': nrm((N_A_LAYERS, G, C, P), P ** -0.5),
        'ssm_c_im': nrm((N_A_LAYERS, G, C, P), P ** -0.5),
        'ssm_d': nrm((N_A_LAYERS, D_MODEL), 1.0),
        'ssm_w_glu': nrm((N_A_LAYERS, D_MODEL, 2 * D_MODEL), D_MODEL ** -0.5),
        'w_kv': nrm((D_MODEL, 2 * KV_DIM), D_MODEL ** -0.5),
        'b_kv': nrm((2 * KV_DIM,), 0.01),
        'w_q': nrm((N_B_LAYERS, D_MODEL, D_MODEL), D_MODEL ** -0.5),
        'b_q': nrm((N_B_LAYERS, D_MODEL), 0.01),
        'attn_sinks': nrm((N_B_LAYERS, N_Q_HEADS), 1.0),
        'w_o': nrm((N_B_LAYERS, D_MODEL, D_MODEL), D_MODEL ** -0.5),
        'b_o': nrm((N_B_LAYERS, D_MODEL), 0.01),
        'moe_w_grp': nrm((DEPTH, D_MODEL, N_EXPERT_GROUPS), D_MODEL ** -0.5),
        'moe_b_grp': nrm((DEPTH, N_EXPERT_GROUPS), 0.01),
        'moe_w_exp': nrm((DEPTH, D_MODEL, N_EXPERTS), D_MODEL ** -0.5),
        'moe_b_exp': nrm((DEPTH, N_EXPERTS), 0.01),
        'moe_w_gate': nrm((DEPTH, N_EXPERTS, D_MODEL, D_EXPERT), D_MODEL ** -0.5),
        'moe_w_up': nrm((DEPTH, N_EXPERTS, D_MODEL, D_EXPERT), D_MODEL ** -0.5),
        'moe_w_down': nrm((DEPTH, N_EXPERTS, D_EXPERT, D_MODEL), D_EXPERT ** -0.5),
    }


def reference(x, meta_tokens, norm_mix, norm_ffn, norm_kv, norm_final,
              ssm_a_re, ssm_a_im, ssm_log_dt, ssm_b_re, ssm_b_im, ssm_c_re, ssm_c_im,
              ssm_d, ssm_w_glu, w_kv, b_kv, w_q, b_q, attn_sinks, w_o, b_o,
              moe_w_grp, moe_b_grp, moe_w_exp, moe_b_exp, moe_w_gate, moe_w_up, moe_w_down):
    bsz = x.shape[0]
    meta = jnp.broadcast_to(meta_tokens[None].astype(x.dtype), (bsz, N_META, D_MODEL))
    h = jnp.concatenate([meta, x], axis=1)
    shared = None
    for layer in range(DEPTH):
        if layer < N_A_LAYERS:
            i = layer
            h = h + s5_mixer(rmsnorm(h, norm_mix[layer]), ssm_a_re[i], ssm_a_im[i], ssm_log_dt[i],
                             ssm_b_re[i], ssm_b_im[i], ssm_c_re[i], ssm_c_im[i],
                             ssm_d[i], ssm_w_glu[i])
        else:
            if shared is None:
                shared = shared_kv(rmsnorm(h, norm_kv), w_kv, b_kv)
            j = layer - N_A_LAYERS
            k_band, v_band, k_meta, v_meta = shared
            h = h + swa_sink_attention(rmsnorm(h, norm_mix[layer]), w_q[j], b_q[j], attn_sinks[j],
                                       w_o[j], b_o[j], k_band, v_band, k_meta, v_meta)
        h = h + hier_moe(rmsnorm(h, norm_ffn[layer]), moe_w_grp[layer], moe_b_grp[layer],
                         moe_w_exp[layer], moe_b_exp[layer], moe_w_gate[layer],
                         moe_w_up[layer], moe_w_down[layer])
    h = rmsnorm(h, norm_final)
    return h[:, N_META:]
```

```python
import functools
import math

import jax
import jax.numpy as jnp
from jax import lax
from jax.experimental import pallas as pl
from jax.experimental.pallas import tpu as pltpu

F32 = jnp.float32
BF16 = jnp.bfloat16

D_MODEL = 1024
BATCH = 8
SEQ = 2048
DEPTH = 4
N_META = 16
N_A_LAYERS = DEPTH // 2
SEQ_ALL = N_META + SEQ
SSM_GROUP = 16
SSM_GROUPS = D_MODEL // SSM_GROUP
SSM_STATE = 64
HEAD_DIM = 64
N_Q_HEADS = D_MODEL // HEAD_DIM
N_KV_HEADS = 4
GQA_GROUP = N_Q_HEADS // N_KV_HEADS
GROUP_COLS = GQA_GROUP * HEAD_DIM
WINDOW = 128
NEG = -1e30
N_EXPERT_GROUPS = 4
EXPERTS_PER_GROUP = 8
N_EXPERTS = N_EXPERT_GROUPS * EXPERTS_PER_GROUP
D_EXPERT = D_MODEL // 2
RMS_EPS = 1e-6

LANES = 128
SUBLANES = 8
S5_CHUNK = 16
S5_ROW = S5_CHUNK * SSM_GROUP
S5_GROUPS_PER_STEP = 4
EXPERT_LANE0 = 32
MOE_BLOCK_ROWS = 256
VMEM_LIMIT = 48 * 1024 * 1024


def _params(*sem):
    return pltpu.CompilerParams(dimension_semantics=sem, vmem_limit_bytes=VMEM_LIMIT)


def _row_tile(rows):
    for tm in (512, 384, 256, 128):
        if rows % tm == 0:
            return tm
    raise ValueError(f"unsupported row count {rows}")


def _rms(x, g):
    return x * lax.rsqrt(jnp.mean(x * x, axis=-1, keepdims=True) + RMS_EPS) * g


def _rmsnorm_kernel(h_ref, g_ref, o_ref):
    o_ref[...] = _rms(h_ref[...], g_ref[...]).astype(o_ref.dtype)


def rmsnorm(h, g, out_dtype):
    rows = h.shape[0]
    tm = _row_tile(rows)
    return pl.pallas_call(
        _rmsnorm_kernel,
        grid=(rows // tm,),
        in_specs=[pl.BlockSpec((tm, D_MODEL), lambda i: (i, 0)),
                  pl.BlockSpec((1, D_MODEL), lambda i: (0, 0))],
        out_specs=pl.BlockSpec((tm, D_MODEL), lambda i: (i, 0)),
        out_shape=jax.ShapeDtypeStruct((rows, D_MODEL), out_dtype),
        compiler_params=_params("parallel"),
        name="rmsnorm",
    )(h, g.reshape(1, D_MODEL))


def _norm_matmul_kernel(h_ref, g_ref, w_ref, b_ref, o_ref):
    xn = _rms(h_ref[...], g_ref[...]).astype(BF16)
    o_ref[...] = (jnp.dot(xn, w_ref[...], preferred_element_type=F32) + b_ref[...]).astype(o_ref.dtype)


def norm_matmul(h, g, w, b, out_dtype=BF16):
    rows, n = h.shape[0], w.shape[1]
    tm = _row_tile(rows)
    return pl.pallas_call(
        _norm_matmul_kernel,
        grid=(rows // tm,),
        in_specs=[pl.BlockSpec((tm, D_MODEL), lambda i: (i, 0)),
                  pl.BlockSpec((1, D_MODEL), lambda i: (0, 0)),
                  pl.BlockSpec((D_MODEL, n), lambda i: (0, 0)),
                  pl.BlockSpec((1, n), lambda i: (0, 0))],
        out_specs=pl.BlockSpec((tm, n), lambda i: (i, 0)),
        out_shape=jax.ShapeDtypeStruct((rows, n), out_dtype),
        compiler_params=_params("parallel"),
        name="norm_matmul",
    )(h, g.reshape(1, D_MODEL), w, b.reshape(1, n))


def _matmul_residual_kernel(x_ref, w_ref, b_ref, h_ref, o_ref):
    o_ref[...] = h_ref[...] + jnp.dot(x_ref[...], w_ref[...], preferred_element_type=F32) + b_ref[...]


def matmul_residual(x, w, b, h):
    rows = h.shape[0]
    tm = _row_tile(rows)
    return pl.pallas_call(
        _matmul_residual_kernel,
        grid=(rows // tm,),
        in_specs=[pl.BlockSpec((tm, D_MODEL), lambda i: (i, 0)),
                  pl.BlockSpec((D_MODEL, D_MODEL), lambda i: (0, 0)),
                  pl.BlockSpec((1, D_MODEL), lambda i: (0, 0)),
                  pl.BlockSpec((tm, D_MODEL), lambda i: (i, 0))],
        out_specs=pl.BlockSpec((tm, D_MODEL), lambda i: (i, 0)),
        out_shape=jax.ShapeDtypeStruct((rows, D_MODEL), F32),
        compiler_params=_params("parallel"),
        name="matmul_residual",
    )(x, w, b.reshape(1, D_MODEL), h)


def _s5_tables(a_re, a_im, log_dt, b_re, b_im, c_re, c_im):
    q = S5_CHUNK
    hi = lax.Precision.HIGHEST
    dt = jnp.exp(log_dt)[:, None]
    steps = jnp.arange(q + 1, dtype=F32)[:, None, None]
    mag = jnp.exp(a_re * dt * steps)
    ang = a_im * dt * steps
    pw_re, pw_im = mag * jnp.cos(ang), mag * jnp.sin(ang)
    ab_re, ab_im = pw_re[1] - 1.0, pw_im[1]
    den = a_re * a_re + a_im * a_im
    f_re = (ab_re * a_re + ab_im * a_im) / den
    f_im = (ab_im * a_re - ab_re * a_im) / den
    bb_re = f_re[:, :, None] * b_re - f_im[:, :, None] * b_im
    bb_im = f_re[:, :, None] * b_im + f_im[:, :, None] * b_re
    m_re = pw_re[:, :, :, None] * bb_re - pw_im[:, :, :, None] * bb_im
    m_im = pw_re[:, :, :, None] * bb_im + pw_im[:, :, :, None] * bb_re
    kd = (jnp.einsum('gop,dgpc->dgoc', c_re, m_re[:q], precision=hi)
          - jnp.einsum('gop,dgpc->dgoc', c_im, m_im[:q], precision=hi))
    s_idx = jnp.arange(q)[:, None]
    t_idx = jnp.arange(q)[None, :]
    lag = t_idx - s_idx
    toep = jnp.where((lag >= 0)[:, :, None, None, None], kd[jnp.clip(lag, 0, q - 1)], 0.0)
    toep = toep.transpose(2, 0, 4, 1, 3).reshape(SSM_GROUPS, S5_ROW, S5_ROW)
    win_re = m_re[q - 1 - jnp.arange(q)].transpose(1, 0, 3, 2).reshape(SSM_GROUPS, S5_ROW, SSM_STATE)
    win_im = m_im[q - 1 - jnp.arange(q)].transpose(1, 0, 3, 2).reshape(SSM_GROUPS, S5_ROW, SSM_STATE)
    w1 = jnp.concatenate([toep, win_re, win_im, win_im, win_re], axis=2)
    cp_re = c_re[None] * pw_re[1:, :, None, :] - c_im[None] * pw_im[1:, :, None, :]
    cp_im = c_re[None] * pw_im[1:, :, None, :] + c_im[None] * pw_re[1:, :, None, :]
    wout = jnp.concatenate([cp_re.transpose(1, 3, 0, 2).reshape(SSM_GROUPS, SSM_STATE, S5_ROW),
                            (-cp_im).transpose(1, 3, 0, 2).reshape(SSM_GROUPS, SSM_STATE, S5_ROW)], axis=1)
    a1 = jnp.concatenate([pw_re[q], pw_re[q]], axis=1)
    a2 = jnp.concatenate([-pw_im[q], pw_im[q]], axis=1)
    coef = jnp.concatenate([jnp.broadcast_to(a1[:, None, :], (SSM_GROUPS, SUBLANES, LANES)),
                            jnp.broadcast_to(a2[:, None, :], (SSM_GROUPS, SUBLANES, LANES))], axis=1)
    return w1.astype(BF16), wout.astype(BF16), coef


def _s5_kernel(u_ref, w1_ref, wout_ref, coef_ref, y_ref, r_ref, sp_ref, *, n_chunks):
    gb = u_ref.shape[0]
    for g in range(gb):
        r_ref[g] = jnp.dot(u_ref[g], w1_ref[g], preferred_element_type=F32)
    a1 = [coef_ref[g, 0:SUBLANES, :] for g in range(gb)]
    a2 = [coef_ref[g, SUBLANES:2 * SUBLANES, :] for g in range(gb)]

    def body(k, carry):
        row = pl.multiple_of(k * SUBLANES, SUBLANES)
        out = []
        for g in range(gb):
            s, st = carry[2 * g], carry[2 * g + 1]
            sp_ref[g, pl.ds(row, SUBLANES), :] = s
            s_in = r_ref[g, pl.ds(row, SUBLANES), S5_ROW:S5_ROW + LANES]
            st_in = r_ref[g, pl.ds(row, SUBLANES), S5_ROW + LANES:S5_ROW + 2 * LANES]
            out.append(a1[g] * s + a2[g] * st + s_in)
            out.append(a1[g] * st - a2[g] * s + st_in)
        return tuple(out)

    zero = jnp.zeros((SUBLANES, LANES), F32)
    lax.fori_loop(0, n_chunks, body, (zero,) * (2 * gb))
    for g in range(gb):
        y = r_ref[g, :, 0:S5_ROW] + jnp.dot(sp_ref[g].astype(BF16), wout_ref[g], preferred_element_type=F32)
        y_ref[g] = y.astype(y_ref.dtype)


def s5_scan(u_t, w1, wout, coef):
    rows = u_t.shape[1]
    gb = S5_GROUPS_PER_STEP
    return pl.pallas_call(
        functools.partial(_s5_kernel, n_chunks=rows // BATCH),
        grid=(SSM_GROUPS // gb,),
        in_specs=[pl.BlockSpec((gb, rows, S5_ROW), lambda i: (i, 0, 0)),
                  pl.BlockSpec((gb, S5_ROW, 2 * S5_ROW), lambda i: (i, 0, 0)),
                  pl.BlockSpec((gb, 2 * SSM_STATE, S5_ROW), lambda i: (i, 0, 0)),
                  pl.BlockSpec((gb, 2 * SUBLANES, LANES), lambda i: (i, 0, 0))],
        out_specs=pl.BlockSpec((gb, rows, S5_ROW), lambda i: (i, 0, 0)),
        out_shape=jax.ShapeDtypeStruct((SSM_GROUPS, rows, S5_ROW), BF16),
        scratch_shapes=[pltpu.VMEM((gb, rows, 2 * S5_ROW), F32),
                        pltpu.VMEM((gb, rows, LANES), F32)],
        compiler_params=_params("parallel"),
        name="s5_scan",
    )(u_t, w1, wout, coef)


def _glu_kernel(y_ref, u_ref, h_ref, d_ref, w_ref, o_ref):
    z = jax.nn.gelu(y_ref[...].astype(F32) + d_ref[...] * u_ref[...].astype(F32)).astype(BF16)
    o = jnp.dot(z, w_ref[...], preferred_element_type=F32)
    o_ref[...] = h_ref[...] + o[:, :D_MODEL] * jax.nn.sigmoid(o[:, D_MODEL:])


def s5_glu(y, u, h, d_skip, w_glu):
    rows = h.shape[0]
    tm = _row_tile(rows)
    return pl.pallas_call(
        _glu_kernel,
        grid=(rows // tm,),
        in_specs=[pl.BlockSpec((tm, D_MODEL), lambda i: (i, 0)),
                  pl.BlockSpec((tm, D_MODEL), lambda i: (i, 0)),
                  pl.BlockSpec((tm, D_MODEL), lambda i: (i, 0)),
                  pl.BlockSpec((1, D_MODEL), lambda i: (0, 0)),
                  pl.BlockSpec((D_MODEL, 2 * D_MODEL), lambda i: (0, 0))],
        out_specs=pl.BlockSpec((tm, D_MODEL), lambda i: (i, 0)),
        out_shape=jax.ShapeDtypeStruct((rows, D_MODEL), F32),
        compiler_params=_params("parallel"),
        name="s5_glu",
    )(y, u, h, d_skip.reshape(1, D_MODEL), w_glu)


def s5_layer(h, g_norm, tables, d_skip, w_glu):
    n_chunks = SEQ_ALL // S5_CHUNK
    u = rmsnorm(h, g_norm, BF16)
    u_t = u.reshape(BATCH, n_chunks, S5_CHUNK, SSM_GROUPS, SSM_GROUP).transpose(3, 1, 0, 2, 4)
    y_t = s5_scan(u_t.reshape(SSM_GROUPS, n_chunks * BATCH, S5_ROW), *tables)
    y = y_t.reshape(SSM_GROUPS, n_chunks, BATCH, S5_CHUNK, SSM_GROUP).transpose(2, 1, 3, 0, 4)
    return s5_glu(y.reshape(BATCH * SEQ_ALL, D_MODEL), u, h, d_skip, w_glu)


def _attn_kernel(sink_ref, q_ref, kc_ref, kp_ref, vc_ref, vp_ref, km_ref, vm_ref, o_ref):
    n = pl.program_id(1)
    rows = GQA_GROUP * WINDOW
    row = lax.broadcasted_iota(jnp.int32, (rows, WINDOW), 0)
    q_pos = row & (WINDOW - 1)
    k_pos = lax.broadcasted_iota(jnp.int32, (rows, WINDOW), 1)
    in_cur = k_pos <= q_pos
    head_of_row = lax.broadcasted_iota(jnp.int32, (rows, 1), 0) // WINDOW
    head_of_lane = lax.broadcasted_iota(jnp.int32, (WINDOW, GROUP_COLS), 1) // HEAD_DIM
    nt = (((1,), (1,)), ((), ()))
    scale = HEAD_DIM ** -0.5
    for kh in range(N_KV_HEADS):
        cols = slice(kh * GROUP_COLS, (kh + 1) * GROUP_COLS)
        qk = q_ref[0, :, cols]
        lhs = jnp.concatenate([jnp.where(head_of_lane == g, qk, jnp.zeros_like(qk)) for g in range(GQA_GROUP)],
                              axis=0)
        s_cur = lax.dot_general(lhs, kc_ref[0, :, cols], nt, preferred_element_type=F32) * scale
        s_prev = lax.dot_general(lhs, kp_ref[0, :, cols], nt, preferred_element_type=F32) * scale
        s_meta = lax.dot_general(lhs, km_ref[0, :, cols], nt, preferred_element_type=F32) * scale
        s_band = jnp.where(in_cur, s_cur, jnp.where(n > 0, s_prev, NEG))
        sink = jnp.zeros((rows, 1), F32)
        for g in range(GQA_GROUP):
            sink = jnp.where(head_of_row == g, sink_ref[kh * GQA_GROUP + g], sink)
        m = jnp.maximum(jnp.maximum(jnp.max(s_band, axis=1, keepdims=True),
                                    jnp.max(s_meta, axis=1, keepdims=True)), sink)
        p = jnp.exp(s_band - m)
        pm = jnp.exp(s_meta - m)
        den = jnp.exp(sink - m) + jnp.sum(p, axis=1, keepdims=True) + jnp.sum(pm, axis=1, keepdims=True)
        p_cat = jnp.concatenate([jnp.where(in_cur, 0.0, p), jnp.where(in_cur, p, 0.0)], axis=1).astype(BF16)
        v_cat = jnp.concatenate([vp_ref[0, :, cols], vc_ref[0, :, cols]], axis=0)
        r = (jnp.dot(p_cat, v_cat, preferred_element_type=F32)
             + jnp.dot(pm.astype(BF16), vm_ref[0, :, cols], preferred_element_type=F32)) / den
        o = jnp.zeros((WINDOW, GROUP_COLS), F32)
        for g in range(GQA_GROUP):
            o = jnp.where(head_of_lane == g, r[g * WINDOW:(g + 1) * WINDOW, :], o)
        o_ref[0, :, cols] = o.astype(o_ref.dtype)


def swa_attention(q, kv, kv_meta, sinks):
    nb = SEQ // WINDOW
    blk = lambda b, n, s: (b, n, 0)
    k_cur = lambda b, n, s: (b, n, 0)
    k_prev = lambda b, n, s: (b, jnp.maximum(n - 1, 0), 0)
    v_cur = lambda b, n, s: (b, n, 1)
    v_prev = lambda b, n, s: (b, jnp.maximum(n - 1, 0), 1)
    return pl.pallas_call(
        _attn_kernel,
        grid_spec=pltpu.PrefetchScalarGridSpec(
            num_scalar_prefetch=1,
            grid=(BATCH, nb),
            in_specs=[pl.BlockSpec((1, WINDOW, D_MODEL), blk),
                      pl.BlockSpec((1, WINDOW, D_MODEL), k_cur),
                      pl.BlockSpec((1, WINDOW, D_MODEL), k_prev),
                      pl.BlockSpec((1, WINDOW, D_MODEL), v_cur),
                      pl.BlockSpec((1, WINDOW, D_MODEL), v_prev),
                      pl.BlockSpec((1, N_META, D_MODEL), lambda b, n, s: (b, 0, 0)),
                      pl.BlockSpec((1, N_META, D_MODEL), lambda b, n, s: (b, 0, 1))],
            out_specs=pl.BlockSpec((1, WINDOW, D_MODEL), blk)),
        out_shape=jax.ShapeDtypeStruct((BATCH, SEQ, D_MODEL), BF16),
        compiler_params=_params("parallel", "parallel"),
        name="swa_attention",
    )(sinks, q, kv, kv, kv, kv, kv_meta, kv_meta)


def _repeat_kv_cols(w):
    lead = w.shape[:-1]
    w = w.reshape(lead + (N_KV_HEADS, 1, HEAD_DIM))
    return jnp.broadcast_to(w, lead + (N_KV_HEADS, GQA_GROUP, HEAD_DIM)).reshape(lead + (D_MODEL,))


def _router_kernel(h_ref, g_ref, w_ref, b_ref, t_ref, info_ref, cnt_ref, carry_ref):
    i = pl.program_id(0)
    tm = h_ref.shape[0]

    @pl.when(i == 0)
    def _():
        carry_ref[...] = jnp.zeros_like(carry_ref)

    t = _rms(h_ref[...], g_ref[...])
    t_ref[...] = t.astype(t_ref.dtype)
    logits = jnp.dot(t, w_ref[...], preferred_element_type=F32, precision=lax.Precision.HIGHEST) + b_ref[...]
    lane = lax.broadcasted_iota(jnp.int32, (tm, LANES), 1)
    gl = jnp.where(lane < N_EXPERT_GROUPS, logits, -jnp.inf)
    gmax = jnp.max(gl, axis=1, keepdims=True)
    p_g = 1.0 / jnp.sum(jnp.exp(gl - gmax), axis=1, keepdims=True)
    g_idx = jnp.min(jnp.where(gl == gmax, lane, LANES), axis=1, keepdims=True)
    lo = EXPERT_LANE0 + EXPERTS_PER_GROUP * g_idx
    el = jnp.where((lane >= lo) & (lane < lo + EXPERTS_PER_GROUP), logits, -jnp.inf)
    m1 = jnp.max(el, axis=1, keepdims=True)
    i1 = jnp.min(jnp.where(el == m1, lane, LANES), axis=1, keepdims=True)
    el2 = jnp.where(lane == i1, -jnp.inf, el)
    m2 = jnp.max(el2, axis=1, keepdims=True)
    i2 = jnp.min(jnp.where(el2 == m2, lane, LANES), axis=1, keepdims=True)
    r21 = jnp.exp(m2 - m1)
    q1 = 1.0 / (1.0 + r21)
    gate1 = p_g * q1
    gate2 = p_g * (r21 * q1)
    onehot = ((lane == i1) | (lane == i2)).astype(BF16)
    r_i = lax.broadcasted_iota(jnp.int32, (tm, tm), 0)
    c_i = lax.broadcasted_iota(jnp.int32, (tm, tm), 1)
    before = (c_i < r_i).astype(BF16)
    seen = jnp.dot(before, onehot, preferred_element_type=F32) + carry_ref[0:1, :]
    rank1 = jnp.sum(jnp.where(lane == i1, seen, 0.0), axis=1, keepdims=True)
    rank2 = jnp.sum(jnp.where(lane == i2, seen, 0.0), axis=1, keepdims=True)
    total = carry_ref[...] + jnp.sum(onehot.astype(F32), axis=0, keepdims=True)
    carry_ref[...] = total
    cnt_ref[...] = total
    info = jnp.where(lane == 0, (i1 - EXPERT_LANE0).astype(F32), 0.0)
    info = jnp.where(lane == 1, (i2 - EXPERT_LANE0).astype(F32), info)
    info = jnp.where(lane == 2, gate1, info)
    info = jnp.where(lane == 3, gate2, info)
    info = jnp.where(lane == 4, rank1, info)
    info = jnp.where(lane == 5, rank2, info)
    info_ref[...] = info


def moe_router(h, g_norm, w_r, b_r):
    rows = h.shape[0]
    tm = _row_tile(rows)
    return pl.pallas_call(
        _router_kernel,
        grid=(rows // tm,),
        in_specs=[pl.BlockSpec((tm, D_MODEL), lambda i: (i, 0)),
                  pl.BlockSpec((1, D_MODEL), lambda i: (0, 0)),
                  pl.BlockSpec((D_MODEL, LANES), lambda i: (0, 0)),
                  pl.BlockSpec((1, LANES), lambda i: (0, 0))],
        out_specs=[pl.BlockSpec((tm, D_MODEL), lambda i: (i, 0)),
                   pl.BlockSpec((tm, LANES), lambda i: (i, 0)),
                   pl.BlockSpec((SUBLANES, LANES), lambda i: (0, 0))],
        out_shape=[jax.ShapeDtypeStruct((rows, D_MODEL), BF16),
                   jax.ShapeDtypeStruct((rows, LANES), F32),
                   jax.ShapeDtypeStruct((SUBLANES, LANES), F32)],
        scratch_shapes=[pltpu.VMEM((SUBLANES, LANES), F32)],
        compiler_params=_params("arbitrary"),
        name="moe_router",
    )(h, g_norm.reshape(1, D_MODEL), w_r, b_r)


def _expert_kernel(blk_e_ref, blk_new_ref, blk_valid_ref, x_ref, wg_ref, wu_ref, wd_ref, y_ref,
                   wg_s, wu_s, wd_s):
    i = pl.program_id(0)

    @pl.when(blk_new_ref[i] != 0)
    def _():
        wg_s[...] = wg_ref[0, 0].astype(BF16)
        wu_s[...] = wu_ref[0, 0].astype(BF16)
        wd_s[...] = wd_ref[0, 0].astype(BF16)

    @pl.when(blk_valid_ref[i] != 0)
    def _():
        x = x_ref[...]
        a = jnp.dot(x, wg_s[...], preferred_element_type=F32)
        b = jnp.dot(x, wu_s[...], preferred_element_type=F32)
        hid = (jax.nn.silu(a) * b).astype(BF16)
        y_ref[...] = jnp.dot(hid, wd_s[...], preferred_element_type=F32).astype(y_ref.dtype)

    @pl.when(blk_valid_ref[i] == 0)
    def _():
        y_ref[...] = jnp.zeros_like(y_ref)


def moe_experts(xb, blk_e, blk_new, blk_valid, w_gate, w_up, w_down, layer):
    n_rows = xb.shape[0]
    bm = MOE_BLOCK_ROWS
    w_in = lambda i, be, bn, bv: (layer, be[i], 0, 0)
    return pl.pallas_call(
        _expert_kernel,
        grid_spec=pltpu.PrefetchScalarGridSpec(
            num_scalar_prefetch=3,
            grid=(n_rows // bm,),
            in_specs=[pl.BlockSpec((bm, D_MODEL), lambda i, be, bn, bv: (i, 0)),
                      pl.BlockSpec((1, 1, D_MODEL, D_EXPERT), w_in),
                      pl.BlockSpec((1, 1, D_MODEL, D_EXPERT), w_in),
                      pl.BlockSpec((1, 1, D_EXPERT, D_MODEL), w_in)],
            out_specs=pl.BlockSpec((bm, D_MODEL), lambda i, be, bn, bv: (i, 0)),
            scratch_shapes=[pltpu.VMEM((D_MODEL, D_EXPERT), BF16),
                            pltpu.VMEM((D_MODEL, D_EXPERT), BF16),
                            pltpu.VMEM((D_EXPERT, D_MODEL), BF16)]),
        out_shape=jax.ShapeDtypeStruct((n_rows, D_MODEL), BF16),
        compiler_params=_params("arbitrary"),
        name="moe_experts",
    )(blk_e, blk_new, blk_valid, xb, w_gate, w_up, w_down)


def _combine_kernel(h_ref, y1_ref, y2_ref, info_ref, o_ref):
    g1 = info_ref[:, 2:3]
    g2 = info_ref[:, 3:4]
    o_ref[...] = h_ref[...] + y1_ref[...].astype(F32) * g1 + y2_ref[...].astype(F32) * g2


def moe_combine(h, y1, y2, info):
    rows = h.shape[0]
    tm = _row_tile(rows)
    row_blk = pl.BlockSpec((tm, D_MODEL), lambda i: (i, 0))
    return pl.pallas_call(
        _combine_kernel,
        grid=(rows // tm,),
        in_specs=[row_blk, row_blk, row_blk, pl.BlockSpec((tm, LANES), lambda i: (i, 0))],
        out_specs=row_blk,
        out_shape=jax.ShapeDtypeStruct((rows, D_MODEL), F32),
        compiler_params=_params("parallel"),
        name="moe_combine",
    )(h, y1, y2, info)


def moe_layer(h, layer, g_norm, w_grp, b_grp, w_exp, b_exp, w_gate, w_up, w_down):
    rows = h.shape[0]
    bm = MOE_BLOCK_ROWS
    w_r = jnp.zeros((D_MODEL, LANES), F32)
    w_r = w_r.at[:, :N_EXPERT_GROUPS].set(w_grp).at[:, EXPERT_LANE0:EXPERT_LANE0 + N_EXPERTS].set(w_exp)
    b_r = jnp.zeros((1, LANES), F32)
    b_r = b_r.at[0, :N_EXPERT_GROUPS].set(b_grp).at[0, EXPERT_LANE0:EXPERT_LANE0 + N_EXPERTS].set(b_exp)
    t, info, cnt = moe_router(h, g_norm, w_r, b_r)

    counts = cnt[0, EXPERT_LANE0:EXPERT_LANE0 + N_EXPERTS].astype(jnp.int32)
    padded = (counts + bm - 1) // bm * bm
    pad_end = jnp.cumsum(padded)
    pad_start = pad_end - padded
    e1 = info[:, 0].astype(jnp.int32)
    e2 = info[:, 1].astype(jnp.int32)
    dest1 = pad_start[e1] + info[:, 4].astype(jnp.int32)
    dest2 = pad_start[e2] + info[:, 5].astype(jnp.int32)
    n_blk = (2 * rows) // bm + N_EXPERTS
    blk_row0 = jnp.arange(n_blk, dtype=jnp.int32) * bm
    blk_e = jnp.minimum(jnp.searchsorted(pad_end, blk_row0, side='right'), N_EXPERTS - 1).astype(jnp.int32)
    blk_valid = (blk_row0 < pad_end[-1]).astype(jnp.int32)
    blk_new = jnp.concatenate([jnp.ones((1,), jnp.int32), (blk_e[1:] != blk_e[:-1]).astype(jnp.int32)])

    tok = jnp.arange(rows, dtype=jnp.int32)
    row_tok = jnp.full((n_blk * bm,), rows, jnp.int32).at[dest1].set(tok).at[dest2].set(tok)
    xb = jnp.concatenate([t, jnp.zeros((1, D_MODEL), t.dtype)], axis=0)[row_tok]
    yb = moe_experts(xb, blk_e, blk_new, blk_valid, w_gate, w_up, w_down, layer)
    return moe_combine(h, yb[dest1], yb[dest2], info)


def kernel(x, meta_tokens, norm_mix, norm_ffn, norm_kv, norm_final, ssm_a_re, ssm_a_im, ssm_log_dt, ssm_b_re, ssm_b_im, ssm_c_re, ssm_c_im, ssm_d, ssm_w_glu, w_kv, b_kv, w_q, b_q, attn_sinks, w_o, b_o, moe_w_grp, moe_b_grp, moe_w_exp, moe_b_exp, moe_w_gate, moe_w_up, moe_w_down):
    assert x.shape == (BATCH, SEQ, D_MODEL)
    meta = jnp.broadcast_to(meta_tokens[None].astype(x.dtype), (BATCH, N_META, D_MODEL))
    h = jnp.concatenate([meta, x], axis=1).reshape(BATCH * SEQ_ALL, D_MODEL)

    def moe(h, layer):
        return moe_layer(h, layer, norm_ffn[layer], moe_w_grp[layer], moe_b_grp[layer], moe_w_exp[layer],
                         moe_b_exp[layer], moe_w_gate, moe_w_up, moe_w_down)

    for layer in range(N_A_LAYERS):
        tables = _s5_tables(ssm_a_re[layer], ssm_a_im[layer], ssm_log_dt[layer], ssm_b_re[layer],
                            ssm_b_im[layer], ssm_c_re[layer], ssm_c_im[layer])
        h = s5_layer(h, norm_mix[layer], tables, ssm_d[layer], ssm_w_glu[layer].astype(BF16))
        h = moe(h, layer)

    h3 = h.reshape(BATCH, SEQ_ALL, D_MODEL)
    h_meta = h3[:, :N_META].reshape(BATCH * N_META, D_MODEL)
    h = h3[:, N_META:].reshape(BATCH * SEQ, D_MODEL)
    w_k, w_v = jnp.split(w_kv, 2, axis=-1)
    b_k, b_v = jnp.split(b_kv, 2, axis=-1)
    w_kv_rep = jnp.concatenate([_repeat_kv_cols(w_k), _repeat_kv_cols(w_v)], axis=-1).astype(BF16)
    b_kv_rep = jnp.concatenate([_repeat_kv_cols(b_k), _repeat_kv_cols(b_v)], axis=-1)
    kv = norm_matmul(h, norm_kv, w_kv_rep, b_kv_rep).reshape(BATCH, SEQ, 2 * D_MODEL)
    kv_meta = norm_matmul(h_meta, norm_kv, w_kv_rep, b_kv_rep).reshape(BATCH, N_META, 2 * D_MODEL)

    for layer in range(N_A_LAYERS, DEPTH):
        j = layer - N_A_LAYERS
        q = norm_matmul(h, norm_mix[layer], w_q[j].astype(BF16), b_q[j]).reshape(BATCH, SEQ, D_MODEL)
        o = swa_attention(q, kv, kv_meta, attn_sinks[j])
        h = matmul_residual(o.reshape(BATCH * SEQ, D_MODEL), w_o[j].astype(BF16), b_o[j], h)
        h = moe(h, layer)

    return rmsnorm(h, norm_final, F32).reshape(BATCH, SEQ, D_MODEL)
```
